```python
import math
import jax, jax.numpy as jnp
from jax import lax
import numpy as np

D_MODEL = 1024
BATCH = 16
SEQ = 2048
DEPTH = 1

N_HEADS = 8
HEAD_DIM = 64
ATTN_WIDTH = N_HEADS * HEAD_DIM
KV_DIM = HEAD_DIM
N_IDX_HEADS = 8
IDX_DIM = 32
TOPK_KEYS_MAX = 256
Q_BLOCK = 128
POOL_WINDOWS = (2, 4, 8, 16)
N_POOL_GROUPS = 4
POOL_GROUP_DIM = 128
POOL_WIDTH = N_POOL_GROUPS * POOL_GROUP_DIM
N_BRANCHES = 2
IN_COLS = ATTN_WIDTH + 2 * KV_DIM + N_IDX_HEADS * IDX_DIM + IDX_DIM + N_IDX_HEADS + POOL_WIDTH + N_BRANCHES * D_MODEL
N_EXPERTS = 32
EXPERT_TOP_K = 4
D_EXPERT = D_MODEL
SWIGLU_ALPHA = 1.702
SWIGLU_LIMIT = 7.0
EXPERT_BLOCK = 128
RMS_EPS = 1e-5
NEG_INF = -1e30

kernel_name = "hybrid_dsa_pool_moe_block"


def rmsnorm(x, g):
    xf = x.astype(jnp.float32)
    y = xf * lax.rsqrt(jnp.mean(xf * xf, axis=-1, keepdims=True) + RMS_EPS)
    return (y * g.astype(jnp.float32)).astype(x.dtype)


def split_columns(proj):
    sizes = (ATTN_WIDTH, KV_DIM, KV_DIM, N_IDX_HEADS * IDX_DIM, IDX_DIM, N_IDX_HEADS,
             POOL_WIDTH, D_MODEL, D_MODEL)
    offs = [int(o) for o in np.cumsum(sizes)[:-1]]
    return jnp.split(proj, offs, axis=-1)


def dsa_attention(q, k, v, iq, ik, iw):
    B, S = q.shape[0], q.shape[1]
    n_keys = min(TOPK_KEYS_MAX, S // 4)
    nqb = S // Q_BLOCK
    f32 = jnp.float32
    key_pos = jnp.arange(S)
    ikf = ik.astype(f32)
    gather_rows = jax.vmap(lambda a, i: a[i])

    def to_blocks(a):
        return jnp.moveaxis(a.reshape((B, nqb, Q_BLOCK) + a.shape[2:]), 1, 0)

    def one_block(args):
        qb, iqb, iwb, q0 = args
        qpos = q0 + jnp.arange(Q_BLOCK)
        causal = key_pos[None, :] <= qpos[:, None]
        rel = jax.nn.relu(jnp.einsum('bqhd,bsd->bqhs', iqb.astype(f32), ikf))
        score = jnp.einsum('bqh,bqhs->bqs', iwb.astype(f32), rel)
        score = jnp.where(causal[None], score, NEG_INF)
        _, sel = lax.top_k(score, n_keys)
        ks = gather_rows(k, sel).astype(f32)
        vs = gather_rows(v, sel).astype(f32)
        valid = sel <= qpos[None, :, None]
        logits = jnp.einsum('bqhd,bqkd->bqhk', qb.astype(f32), ks) * (HEAD_DIM ** -0.5)
        logits = jnp.where(valid[:, :, None, :], logits, NEG_INF)
        p = jax.nn.softmax(logits, axis=-1)
        return jnp.einsum('bqhk,bqkd->bqhd', p, vs).astype(q.dtype)

    out = lax.map(one_block, (to_blocks(q), to_blocks(iq), to_blocks(iw),
                              jnp.arange(nqb) * Q_BLOCK))
    return jnp.moveaxis(out, 0, 1).reshape(B, S, ATTN_WIDTH)


def multiscale_pool(u, w_pool, pool_scale):
    B, S, _ = u.shape
    uf = u.astype(jnp.float32)
    csum = jnp.cumsum(uf, axis=1)
    t = jnp.arange(S)
    means = []
    for gi, win in enumerate(POOL_WINDOWS):
        cg = csum[..., gi * POOL_GROUP_DIM:(gi + 1) * POOL_GROUP_DIM]
        shifted = jnp.pad(cg, ((0, 0), (win, 0), (0, 0)))[:, :S]
        count = jnp.minimum(t + 1, win).astype(jnp.float32)[None, :, None]
        means.append((cg - shifted) / count)
    pooled = jnp.stack(means, axis=2) - uf.reshape(B, S, N_POOL_GROUPS, POOL_GROUP_DIM)
    mixed = jnp.einsum('bsgc,gcd->bsgd', pooled, w_pool.astype(jnp.float32)).reshape(B, S, POOL_WIDTH)
    return (mixed * pool_scale.astype(jnp.float32)).astype(u.dtype)


def expert_ffn(h, w_router, b_router, w_gate_up, b_gate_up, w_down, b_down):
    B, S, D = h.shape
    T = B * S
    f32 = jnp.float32
    xt = h.reshape(T, D)
    logits = xt.astype(f32) @ w_router.astype(f32) + b_router.astype(f32)
    top_val, top_idx = lax.top_k(logits, EXPERT_TOP_K)
    gate = jax.nn.softmax(top_val, axis=-1)
    A = T * EXPERT_TOP_K
    flat_e = top_idx.reshape(A)
    flat_tok = jnp.arange(A, dtype=jnp.int32) // EXPERT_TOP_K
    flat_g = gate.reshape(A)
    order = jnp.argsort(flat_e)
    se = flat_e[order]
    counts = jnp.bincount(flat_e, length=N_EXPERTS)
    blocks_per = (counts + EXPERT_BLOCK - 1) // EXPERT_BLOCK
    block_end = jnp.cumsum(blocks_per)
    block_start = block_end - blocks_per
    row_start = jnp.cumsum(counts) - counts
    rank = jnp.arange(A) - row_start[se]
    dest = block_start[se] * EXPERT_BLOCK + rank
    nb = A // EXPERT_BLOCK + N_EXPERTS
    P = nb * EXPERT_BLOCK
    row_tok = jnp.full((P,), T, jnp.int32).at[dest].set(flat_tok[order])
    row_g = jnp.zeros((P,), f32).at[dest].set(flat_g[order])
    blk_e = jnp.minimum(jnp.searchsorted(block_end, jnp.arange(nb), side='right'), N_EXPERTS - 1)
    xs = jnp.take(xt, row_tok, axis=0, mode='fill', fill_value=0).reshape(nb, EXPERT_BLOCK, D)

    def expert_block(args):
        xb, e = args
        gu = xb.astype(f32) @ w_gate_up[e].astype(f32) + b_gate_up[e].astype(f32)
        g, u = gu[:, 0::2], gu[:, 1::2]
        g = jnp.minimum(g, SWIGLU_LIMIT)
        u = jnp.clip(u, -SWIGLU_LIMIT, SWIGLU_LIMIT)
        act = (u + 1.0) * (g * jax.nn.sigmoid(SWIGLU_ALPHA * g))
        return act @ w_down[e].astype(f32) + b_down[e].astype(f32)

    ys = lax.map(expert_block, (xs, blk_e)).reshape(P, D)
    out = jax.ops.segment_sum(ys * row_g[:, None], row_tok, num_segments=T)
    return out.reshape(B, S, D).astype(h.dtype)


def setup_inputs(seed: int = 0) -> dict:
    key = jax.random.key(seed)
    ks = jax.random.split(key, 17)
    n = jax.random.normal
    f = jnp.float32
    F2 = 2 * D_EXPERT
    return {
        "x": n(ks[0], (BATCH, SEQ, D_MODEL), f),
        "norm_mix": 1.0 + 0.1 * n(ks[1], (DEPTH, D_MODEL), f),
        "w_in": n(ks[2], (DEPTH, D_MODEL, IN_COLS), f) * D_MODEL ** -0.5,
        "w_pool": n(ks[3], (DEPTH, N_POOL_GROUPS, POOL_GROUP_DIM, POOL_GROUP_DIM), f) * POOL_GROUP_DIM ** -0.5,
        "pool_scale": 1.0 + 0.1 * n(ks[4], (DEPTH, POOL_WIDTH), f),
        "w_attn_up": n(ks[5], (DEPTH, ATTN_WIDTH, D_MODEL), f) * ATTN_WIDTH ** -0.5,
        "w_pool_up": n(ks[6], (DEPTH, POOL_WIDTH, D_MODEL), f) * POOL_WIDTH ** -0.5,
        "w_out": n(ks[7], (DEPTH, D_MODEL, D_MODEL), f) * D_MODEL ** -0.5,
        "norm_ffn": 1.0 + 0.1 * n(ks[8], (DEPTH, D_MODEL), f),
        "w_router": n(ks[9], (DEPTH, D_MODEL, N_EXPERTS), f) * D_MODEL ** -0.5,
        "b_router": 0.01 * n(ks[10], (DEPTH, N_EXPERTS), f),
        "w_gate_up": n(ks[11], (DEPTH, N_EXPERTS, D_MODEL, F2), f) * D_MODEL ** -0.5,
        "b_gate_up": 0.02 * n(ks[12], (DEPTH, N_EXPERTS, F2), f),
        "w_down": n(ks[13], (DEPTH, N_EXPERTS, D_EXPERT, D_MODEL), f) * D_EXPERT ** -0.5,
        "b_down": 0.02 * n(ks[14], (DEPTH, N_EXPERTS, D_MODEL), f),
        "norm_final": 1.0 + 0.1 * n(ks[15], (D_MODEL,), f),
    }


def reference(x, norm_mix, w_in, w_pool, pool_scale, w_attn_up, w_pool_up, w_out, norm_ffn,
              w_router, b_router, w_gate_up, b_gate_up, w_down, b_down, norm_final):
    B, S, _ = x.shape
    idx_w_scale = (N_IDX_HEADS ** -0.5) * (IDX_DIM ** -0.5)
    for l in range(DEPTH):
        h = rmsnorm(x, norm_mix[l])
        proj = h @ w_in[l]
        q, k, v, iq, ik, iw, u, ga, gp = split_columns(proj)
        q = q.reshape(B, S, N_HEADS, HEAD_DIM)
        iq = iq.reshape(B, S, N_IDX_HEADS, IDX_DIM)
        iw = iw * idx_w_scale
        y_attn = dsa_attention(q, k, v, iq, ik, iw)
        y_pool = multiscale_pool(u, w_pool[l], pool_scale[l])
        mixed = (jax.nn.sigmoid(ga) * (y_attn @ w_attn_up[l])
                 + jax.nn.sigmoid(gp) * (y_pool @ w_pool_up[l]))
        x = x + mixed @ w_out[l]
        h2 = rmsnorm(x, norm_ffn[l])
        x = x + expert_ffn(h2, w_router[l], b_router[l], w_gate_up[l], b_gate_up[l], w_down[l], b_down[l])
    return rmsnorm(x, norm_final)
```

```python
import functools

import jax
import jax.numpy as jnp
from jax import lax
from jax.experimental import pallas as pl
from jax.experimental.pallas import tpu as pltpu

N_HEADS = 8
HEAD_DIM = 64
ATTN_WIDTH = N_HEADS * HEAD_DIM
KV_DIM = HEAD_DIM
N_IDX_HEADS = 8
IDX_DIM = 32
TOPK_KEYS_MAX = 256
POOL_WINDOWS = (2, 4, 8, 16)
N_POOL_GROUPS = 4
POOL_GROUP_DIM = 128
POOL_WIDTH = N_POOL_GROUPS * POOL_GROUP_DIM
N_EXPERTS = 32
EXPERT_TOP_K = 4
SWIGLU_ALPHA = 1.702
SWIGLU_LIMIT = 7.0
RMS_EPS = 1e-5
NEG_INF = -1e30

LANES = 128
SUBLANES = 8
POOL_HALO = 16
INT_MIN = -(2 ** 31)
VMEM_LIMIT = 56 * 1024 * 1024

F32 = jnp.float32
BF16 = jnp.bfloat16


def _nt_dot(a, b):
    return lax.dot_general(a, b, (((1,), (1,)), ((), ())), preferred_element_type=F32)


def _dot(a, b):
    return jnp.dot(a, b, preferred_element_type=F32)


def _rms(x, g):
    return x * lax.rsqrt(jnp.mean(x * x, axis=-1, keepdims=True) + RMS_EPS) * g


_C_Q = 0
_C_KV = _C_Q + ATTN_WIDTH
_C_IQ = _C_KV + 2 * KV_DIM
_C_IKW = _C_IQ + N_IDX_HEADS * IDX_DIM
_C_U = _C_IKW + LANES
_C_G = _C_U + POOL_WIDTH
IKW_PAD = LANES - IDX_DIM - N_IDX_HEADS


def _proj_kernel(d_model, x_ref, g_ref, w_ref, q_ref, kv_ref, iq_ref, ikw_ref, u_ref, sg_ref):
    hb = _rms(x_ref[...], g_ref[...]).astype(BF16)

    def mm(lo, hi):
        return _dot(hb, w_ref[:, lo:hi])

    q_ref[...] = mm(_C_Q, _C_KV).astype(BF16)
    kv_ref[...] = mm(_C_KV, _C_IQ).astype(BF16)
    iq_ref[...] = mm(_C_IQ, _C_IKW).astype(BF16)
    ikw_ref[...] = mm(_C_IKW, _C_U)
    u_ref[...] = mm(_C_U, _C_G)
    sg_ref[...] = jax.nn.sigmoid(mm(_C_G, _C_G + 2 * d_model)).astype(BF16)


def _proj(x2, g, w_cat, tm):
    T, D = x2.shape
    NP = w_cat.shape[1]
    row = lambda w: pl.BlockSpec((tm, w), lambda i: (i, 0))
    outs = [(ATTN_WIDTH, BF16), (2 * KV_DIM, BF16), (N_IDX_HEADS * IDX_DIM, BF16),
            (LANES, F32), (POOL_WIDTH, F32), (2 * D, BF16)]
    return pl.pallas_call(
        functools.partial(_proj_kernel, D),
        grid=(T // tm,),
        in_specs=[row(D), pl.BlockSpec((1, D), lambda i: (0, 0)),
                  pl.BlockSpec((D, NP), lambda i: (0, 0))],
        out_specs=[row(w) for w, _ in outs],
        out_shape=[jax.ShapeDtypeStruct((T, w), dt) for w, dt in outs],
        compiler_params=pltpu.CompilerParams(dimension_semantics=("arbitrary",),
                                             vmem_limit_bytes=VMEM_LIMIT),
        name="proj",
    )(x2, g, w_cat)


def _attn_body(ke, qb, n_keys, idx_scale, q0, q_ref, kv_ref, iq_ref, ikw_ref, ikwq_ref, o_ref):
    ik = ikw_ref[0:ke, 0:IDX_DIM].astype(BF16)
    iw = ikwq_ref[:, IDX_DIM:IDX_DIM + N_IDX_HEADS] * idx_scale
    iq = iq_ref[...]
    score = jnp.zeros((qb, ke), F32)
    for h in range(N_IDX_HEADS):
        rel = jnp.maximum(_nt_dot(iq[:, h * IDX_DIM:(h + 1) * IDX_DIM], ik), 0.0)
        score = score + iw[:, h:h + 1] * rel

    col = lax.broadcasted_iota(jnp.int32, (qb, ke), 1)
    row = q0 + lax.broadcasted_iota(jnp.int32, (qb, ke), 0)
    causal = col <= row
    bits = lax.bitcast_convert_type(score + 0.0, jnp.int32)
    key = jnp.where(bits < 0, bits ^ jnp.int32(0x7FFFFFFF), bits)
    key = jnp.where(causal, key, jnp.int32(INT_MIN))

    def step(i, thr):
        cand = thr + (jnp.int32(1) << (31 - i))
        cnt = jnp.sum(jnp.where(key >= cand, 1.0, 0.0), axis=1, keepdims=True)
        return jnp.where(cnt >= float(n_keys), cand, thr)

    thr = lax.fori_loop(0, 32, step, jnp.full((qb, 1), INT_MIN, jnp.int32))
    gt = key > thr
    eq = key == thr
    need = float(n_keys) - jnp.sum(jnp.where(gt, 1.0, 0.0), axis=1, keepdims=True)
    eqb = jnp.where(eq, 1.0, 0.0).astype(BF16)
    tri = (lax.broadcasted_iota(jnp.int32, (LANES, LANES), 0)
           <= lax.broadcasted_iota(jnp.int32, (LANES, LANES), 1)).astype(BF16)
    offset = jnp.zeros((qb, 1), F32)
    pref = []
    for c in range(ke // LANES):
        pc = _dot(eqb[:, c * LANES:(c + 1) * LANES], tri) + offset
        pref.append(pc)
        offset = pc[:, LANES - 1:LANES]
    prefix = jnp.concatenate(pref, axis=1) if len(pref) > 1 else pref[0]
    sel = (gt | (eq & (prefix <= need))) & causal

    k = kv_ref[0:ke, 0:KV_DIM]
    v = kv_ref[0:ke, KV_DIM:2 * KV_DIM]
    q = q_ref[...]
    outs = []
    for h in range(N_HEADS):
        logit = _nt_dot(q[:, h * HEAD_DIM:(h + 1) * HEAD_DIM], k)
        logit = jnp.where(sel, logit, NEG_INF)
        m = jnp.max(logit, axis=1, keepdims=True)
        p = jnp.exp(logit - m)
        s = jnp.sum(p, axis=1, keepdims=True)
        outs.append(_dot(p.astype(BF16), v) / s)
    o_ref[...] = jnp.concatenate(outs, axis=1).astype(BF16)


def _attn_kernel(qb, kc, n_cases, n_keys, idx_scale, q_ref, kv_ref, iq_ref, ikw_ref, ikwq_ref, o_ref):
    j = pl.program_id(1)
    q0 = j * qb
    case = (q0 + qb - 1) // kc
    for c in range(n_cases):
        @pl.when(case == c)
        def _():
            _attn_body((c + 1) * kc, qb, n_keys, idx_scale, q0,
                       q_ref, kv_ref, iq_ref, ikw_ref, ikwq_ref, o_ref)


def _attn(q, kv, iq, ikw, B, S, qb, kc):
    n_keys = min(TOPK_KEYS_MAX, S // 4)
    idx_scale = (N_IDX_HEADS ** -0.5) * (IDX_DIM ** -0.5)
    nq = S // qb
    blk = lambda w: pl.BlockSpec((qb, w), lambda b, j: (b * nq + j, 0))
    seq = lambda w: pl.BlockSpec((S, w), lambda b, j: (b, 0))
    return pl.pallas_call(
        functools.partial(_attn_kernel, qb, kc, S // kc, n_keys, idx_scale),
        grid=(B, nq),
        in_specs=[blk(ATTN_WIDTH), seq(2 * KV_DIM), blk(N_IDX_HEADS * IDX_DIM), seq(LANES), blk(LANES)],
        out_specs=blk(ATTN_WIDTH),
        out_shape=jax.ShapeDtypeStruct((B * S, ATTN_WIDTH), BF16),
        compiler_params=pltpu.CompilerParams(dimension_semantics=("arbitrary", "arbitrary"),
                                             vmem_limit_bytes=VMEM_LIMIT),
        name="attn",
    )(q, kv, iq, ikw, ikw)


def _mix_kernel(tm, d_model, x_ref, ya_ref, u_ref, halo_ref, sg_ref, wpool_ref, pscale_ref,
                wa_ref, wp_ref, wo_ref, nf_ref, wr_ref, br_ref,
                x1_ref, h2_ref, aux_ref, cnt_ref, carry_ref):
    b = pl.program_id(0)
    j = pl.program_id(1)

    @pl.when((b == 0) & (j == 0))
    def _():
        carry_ref[...] = jnp.zeros_like(carry_ref)

    u = u_ref[0]
    halo = jnp.where(j == 0, 0.0, halo_ref[0])
    ext = jnp.concatenate([halo, u], axis=0)
    n_ext = tm + POOL_HALO

    def shifted(a, d):
        return jnp.concatenate([jnp.zeros((d, a.shape[1]), F32), a[:n_ext - d]], axis=0)

    t_pos = (j * tm + lax.broadcasted_iota(jnp.int32, (tm, 1), 0) + 1).astype(F32)
    wsum = ext
    pooled = []
    for gi, win in enumerate(POOL_WINDOWS):
        wsum = wsum + shifted(wsum, win // 2)
        lo = gi * POOL_GROUP_DIM
        mean = wsum[POOL_HALO:, lo:lo + POOL_GROUP_DIM] / jnp.minimum(t_pos, float(win))
        pg = (mean - u[:, lo:lo + POOL_GROUP_DIM]).astype(BF16)
        pooled.append(_dot(pg, wpool_ref[gi]))
    y_pool = (jnp.concatenate(pooled, axis=1) * pscale_ref[...]).astype(BF16)

    sg = sg_ref[...]
    mixed = (sg[:, :d_model].astype(F32) * _dot(ya_ref[...], wa_ref[...])
             + sg[:, d_model:].astype(F32) * _dot(y_pool, wp_ref[...]))
    x1 = x_ref[...] + _dot(mixed.astype(BF16), wo_ref[...])
    x1_ref[...] = x1
    h2 = _rms(x1, nf_ref[...])
    h2_ref[...] = h2

    h_hi = h2.astype(BF16)
    h_lo = (h2 - h_hi.astype(F32)).astype(BF16)
    logits = (_dot(h_hi, wr_ref[0]) + _dot(h_lo, wr_ref[0]) + _dot(h_hi, wr_ref[1])) + br_ref[...]

    lane = lax.broadcasted_iota(jnp.int32, (tm, LANES), 1)
    lane_f = lane.astype(F32)
    work = logits
    vals, idxs = [], []
    member = jnp.zeros((tm, LANES), F32)
    for _ in range(EXPERT_TOP_K):
        m = jnp.max(work, axis=1, keepdims=True)
        idx = jnp.min(jnp.where(work == m, lane_f, float(LANES)), axis=1, keepdims=True)
        hit = lane_f == idx
        member = jnp.where(hit, 1.0, member)
        work = jnp.where(hit, -jnp.inf, work)
        vals.append(m)
        idxs.append(idx)
    exps = [jnp.exp(v - vals[0]) for v in vals]
    denom = exps[0] + exps[1] + exps[2] + exps[3]

    tri = (lax.broadcasted_iota(jnp.int32, (tm, tm), 1)
           < lax.broadcasted_iota(jnp.int32, (tm, tm), 0)).astype(BF16)
    rank = _dot(tri, member.astype(BF16)) + carry_ref[...]
    new_carry = carry_ref[...] + jnp.sum(member, axis=0, keepdims=True)
    carry_ref[...] = new_carry
    cnt_ref[...] = new_carry

    aux = jnp.zeros((tm, LANES), F32)
    for s in range(EXPERT_TOP_K):
        r = jnp.sum(jnp.where(lane_f == idxs[s], rank, 0.0), axis=1, keepdims=True)
        aux = jnp.where(lane == s, exps[s] / denom, aux)
        aux = jnp.where(lane == EXPERT_TOP_K + s, idxs[s], aux)
        aux = jnp.where(lane == 2 * EXPERT_TOP_K + s, r, aux)
    aux_ref[...] = aux


def _mix(x2, y_attn, u, sg, w_pool, pool_scale, w_attn_up, w_pool_up, w_out, norm_ffn,
         wr_split, br_pad, B, S, tm):
    T, D = x2.shape
    nt = S // tm
    row = lambda w: pl.BlockSpec((tm, w), lambda b, j: (b * nt + j, 0))
    full = lambda a: pl.BlockSpec(a.shape, lambda b, j: (0,) * a.ndim)
    u3 = u.reshape(B, S, POOL_WIDTH)
    hpb = tm // POOL_HALO
    return pl.pallas_call(
        functools.partial(_mix_kernel, tm, D),
        grid=(B, nt),
        in_specs=[row(D), row(ATTN_WIDTH),
                  pl.BlockSpec((1, tm, POOL_WIDTH), lambda b, j: (b, j, 0)),
                  pl.BlockSpec((1, POOL_HALO, POOL_WIDTH), lambda b, j: (b, jnp.maximum(j * hpb - 1, 0), 0)),
                  row(2 * D), full(w_pool), full(pool_scale), full(w_attn_up), full(w_pool_up),
                  full(w_out), full(norm_ffn), full(wr_split), full(br_pad)],
        out_specs=[row(D), row(D), row(LANES), pl.BlockSpec((1, LANES), lambda b, j: (0, 0))],
        out_shape=[jax.ShapeDtypeStruct((T, D), F32), jax.ShapeDtypeStruct((T, D), F32),
                   jax.ShapeDtypeStruct((T, LANES), F32), jax.ShapeDtypeStruct((1, LANES), F32)],
        scratch_shapes=[pltpu.VMEM((1, LANES), F32)],
        compiler_params=pltpu.CompilerParams(dimension_semantics=("arbitrary", "arbitrary"),
                                             vmem_limit_bytes=VMEM_LIMIT),
        name="mix",
    )(x2, y_attn, u3, u3, sg, w_pool, pool_scale, w_attn_up, w_pool_up, w_out, norm_ffn,
      wr_split, br_pad)


def _dispatch_kernel(td, blk, n_blocks, pad_ref, dest_ref, h_ref, xs_ref, zero_ref, sem):
    i = pl.program_id(0)

    @pl.when(i == 0)
    def _():
        zero_ref[...] = jnp.zeros_like(zero_ref)

        def zero_fill(e):
            start = pl.multiple_of((pad_ref[e] // SUBLANES) * SUBLANES, SUBLANES)
            return pltpu.make_async_copy(zero_ref, xs_ref.at[pl.ds(start, blk + SUBLANES)], sem)

        def zero_block(k):
            start = pl.multiple_of((pad_ref[N_EXPERTS] + k) * blk, blk)
            return pltpu.make_async_copy(zero_ref.at[pl.ds(0, blk)], xs_ref.at[pl.ds(start, blk)], sem)

        n_spare = n_blocks - pad_ref[N_EXPERTS]
        for e in range(N_EXPERTS):
            zero_fill(e).start()
        for k in range(N_EXPERTS + 2):
            @pl.when(k < n_spare)
            def _():
                zero_block(k).start()
        for e in range(N_EXPERTS):
            zero_fill(e).wait()
        for k in range(N_EXPERTS + 2):
            @pl.when(k < n_spare)
            def _():
                zero_block(k).wait()

    def issue(r, carry):
        for s in range(EXPERT_TOP_K):
            d = dest_ref[0, 0, r * EXPERT_TOP_K + s]
            pltpu.make_async_copy(h_ref.at[pl.ds(r, 1)], xs_ref.at[pl.ds(d, 1)], sem).start()
        return carry

    lax.fori_loop(0, td, issue, 0)

    def drain(r, carry):
        for s in range(EXPERT_TOP_K):
            pltpu.make_async_copy(h_ref.at[pl.ds(0, 1)], xs_ref.at[pl.ds(0, 1)], sem).wait()
        return carry

    lax.fori_loop(0, td, drain, 0)


def _dispatch(h2, dest, pad_start, n_blocks, td, blk):
    T, D = h2.shape
    dest3 = dest.reshape(T // td, 1, td * EXPERT_TOP_K)
    return pl.pallas_call(
        functools.partial(_dispatch_kernel, td, blk, n_blocks),
        grid_spec=pltpu.PrefetchScalarGridSpec(
            num_scalar_prefetch=1,
            grid=(T // td,),
            in_specs=[pl.BlockSpec((1, 1, td * EXPERT_TOP_K), lambda i, p: (i, 0, 0),
                                   memory_space=pltpu.SMEM),
                      pl.BlockSpec((td, D), lambda i, p: (i, 0))],
            out_specs=pl.BlockSpec(memory_space=pl.ANY),
            scratch_shapes=[pltpu.VMEM((blk + SUBLANES, D), F32), pltpu.SemaphoreType.DMA(())],
        ),
        out_shape=jax.ShapeDtypeStruct((n_blocks * blk, D), F32),
        compiler_params=pltpu.CompilerParams(dimension_semantics=("arbitrary",),
                                             vmem_limit_bytes=VMEM_LIMIT),
        name="dispatch",
    )(pad_start, dest3, h2)


def _moe_kernel(be_ref, nreal_ref, x_ref, wg_ref, wu_ref, bg_ref, bu_ref, wd_ref, bd_ref, y_ref):
    i = pl.program_id(0)

    @pl.when(i < nreal_ref[0])
    def _():
        xb = x_ref[...].astype(BF16)
        g = _dot(xb, wg_ref[0]) + bg_ref[0]
        u = _dot(xb, wu_ref[0]) + bu_ref[0]
        g = jnp.minimum(g, SWIGLU_LIMIT)
        u = jnp.clip(u, -SWIGLU_LIMIT, SWIGLU_LIMIT)
        act = (u + 1.0) * (g * jax.nn.sigmoid(SWIGLU_ALPHA * g))
        y_ref[...] = _dot(act.astype(BF16), wd_ref[0]) + bd_ref[0]

    @pl.when(i >= nreal_ref[0])
    def _():
        y_ref[...] = jnp.zeros_like(y_ref)


def _moe(xs, blk_e, n_real, wg, wu, bg, bu, wd, bd, nb, blk):
    D = xs.shape[1]
    F = wg.shape[2]
    wspec = lambda k, n: pl.BlockSpec((1, k, n), lambda i, be, nr: (be[i], 0, 0))
    return pl.pallas_call(
        _moe_kernel,
        grid_spec=pltpu.PrefetchScalarGridSpec(
            num_scalar_prefetch=2,
            grid=(nb,),
            in_specs=[pl.BlockSpec((blk, D), lambda i, be, nr: (jnp.minimum(i, nr[0] - 1), 0)),
                      wspec(D, F), wspec(D, F), wspec(1, F), wspec(1, F), wspec(F, D), wspec(1, D)],
            out_specs=pl.BlockSpec((blk, D), lambda i, be, nr: (i, 0)),
        ),
        out_shape=jax.ShapeDtypeStruct((nb * blk, D), F32),
        compiler_params=pltpu.CompilerParams(dimension_semantics=("arbitrary",),
                                             vmem_limit_bytes=VMEM_LIMIT),
        name="moe",
    )(blk_e, n_real, xs, wg, wu, bg, bu, wd, bd)


def _combine_kernel(tc, dest_ref, x1_ref, aux_ref, nf_ref, ys_ref, o_ref, buf_ref, sem):
    def issue(r, carry):
        for s in range(EXPERT_TOP_K):
            d = dest_ref[0, 0, r * EXPERT_TOP_K + s]
            pltpu.make_async_copy(ys_ref.at[pl.ds(d, 1)], buf_ref.at[s, pl.ds(r, 1)], sem).start()
        return carry

    lax.fori_loop(0, tc, issue, 0)

    def drain(r, carry):
        for s in range(EXPERT_TOP_K):
            pltpu.make_async_copy(ys_ref.at[pl.ds(0, 1)], buf_ref.at[s, pl.ds(0, 1)], sem).wait()
        return carry

    lax.fori_loop(0, tc, drain, 0)

    aux = aux_ref[...]
    acc = x1_ref[...]
    for s in range(EXPERT_TOP_K):
        acc = acc + aux[:, s:s + 1] * buf_ref[s]
    o_ref[...] = _rms(acc, nf_ref[...])


def _combine(x1, aux, dest, ys, norm_final, tc):
    T, D = x1.shape
    dest3 = dest.reshape(T // tc, 1, tc * EXPERT_TOP_K)
    return pl.pallas_call(
        functools.partial(_combine_kernel, tc),
        grid=(T // tc,),
        in_specs=[pl.BlockSpec((1, 1, tc * EXPERT_TOP_K), lambda i: (i, 0, 0), memory_space=pltpu.SMEM),
                  pl.BlockSpec((tc, D), lambda i: (i, 0)),
                  pl.BlockSpec((tc, LANES), lambda i: (i, 0)),
                  pl.BlockSpec((1, D), lambda i: (0, 0)),
                  pl.BlockSpec(memory_space=pl.ANY)],
        out_specs=pl.BlockSpec((tc, D), lambda i: (i, 0)),
        out_shape=jax.ShapeDtypeStruct((T, D), F32),
        scratch_shapes=[pltpu.VMEM((EXPERT_TOP_K, tc, D), F32), pltpu.SemaphoreType.DMA(())],
        compiler_params=pltpu.CompilerParams(dimension_semantics=("arbitrary",),
                                             vmem_limit_bytes=VMEM_LIMIT),
        name="combine",
    )(dest3, x1, aux, norm_final, ys)


def _tiles(B, S):
    return dict(
        tm_proj=min(512, S),
        qb=min(128, S),
        kc=min(512, S),
        tm_mix=min(512, S),
        td=min(256, S),
        blk=256,
        tc=min(128, S),
    )


def _prep_w_in(w_in):
    D = w_in.shape[0]
    sizes = (ATTN_WIDTH, KV_DIM, KV_DIM, N_IDX_HEADS * IDX_DIM, IDX_DIM, N_IDX_HEADS, POOL_WIDTH, D, D)
    offs = [0]
    for s in sizes:
        offs.append(offs[-1] + s)
    seg = [w_in[:, offs[i]:offs[i + 1]] for i in range(len(sizes))]
    q, k, v, iq, ik, iw, u, ga, gp = seg
    pad = jnp.zeros((D, IKW_PAD), w_in.dtype)
    return jnp.concatenate([q * (HEAD_DIM ** -0.5), k, v, iq, ik, iw, pad, u, ga, gp], axis=1).astype(BF16)


def _layer(x2, B, S, norm_mix, w_in, w_pool, pool_scale, w_attn_up, w_pool_up, w_out, norm_ffn,
           w_router, b_router, w_gate_up, b_gate_up, w_down, b_down, norm_out):
    T, D = x2.shape
    t = _tiles(B, S)
    blk = t["blk"]
    E = N_EXPERTS
    A = T * EXPERT_TOP_K

    q, kv, iq, ikw, u, sg = _proj(x2, norm_mix.reshape(1, D), _prep_w_in(w_in), t["tm_proj"])
    y_attn = _attn(q, kv, iq, ikw, B, S, t["qb"], t["kc"])

    wr_pad = jnp.pad(w_router, ((0, 0), (0, LANES - E)))
    wr_hi = wr_pad.astype(BF16)
    wr_lo = (wr_pad - wr_hi.astype(F32)).astype(BF16)
    br_pad = jnp.concatenate([b_router, jnp.full((LANES - E,), NEG_INF, F32)]).reshape(1, LANES)
    x1, h2, aux, counts = _mix(
        x2, y_attn, u, sg, w_pool.astype(BF16), pool_scale.reshape(1, POOL_WIDTH),
        w_attn_up.astype(BF16), w_pool_up.astype(BF16), w_out.astype(BF16), norm_ffn.reshape(1, D),
        jnp.stack([wr_hi, wr_lo]), br_pad, B, S, t["tm_mix"])

    counts = counts[0, :E].astype(jnp.int32)
    blocks_per = (counts + blk - 1) // blk
    block_end = jnp.cumsum(blocks_per)
    base = (block_end - blocks_per) * blk
    nb = A // blk + E
    n_real = block_end[-1:].astype(jnp.int32)
    blk_e = jnp.minimum(jnp.searchsorted(block_end, jnp.arange(nb), side="right"), E - 1).astype(jnp.int32)
    top_idx = aux[:, EXPERT_TOP_K:2 * EXPERT_TOP_K].astype(jnp.int32)
    rank = aux[:, 2 * EXPERT_TOP_K:3 * EXPERT_TOP_K].astype(jnp.int32)
    dest = (base[top_idx] + rank).reshape(A)
    pad_start = jnp.concatenate([base + counts, n_real]).astype(jnp.int32)

    xs = _dispatch(h2, dest, pad_start, nb + 2, t["td"], blk)
    ys = _moe(xs, blk_e, n_real,
              w_gate_up[:, :, 0::2].astype(BF16), w_gate_up[:, :, 1::2].astype(BF16),
              b_gate_up[:, None, 0::2], b_gate_up[:, None, 1::2],
              w_down.astype(BF16), b_down[:, None, :], nb, blk)
    return _combine(x1, aux, dest, ys, norm_out.reshape(1, D), t["tc"])


def kernel(x, norm_mix, w_in, w_pool, pool_scale, w_attn_up, w_pool_up, w_out, norm_ffn,
           w_router, b_router, w_gate_up, b_gate_up, w_down, b_down, norm_final):
    B, S, D = x.shape
    depth = w_in.shape[0]
    assert depth == 1, "the final rmsnorm is fused into the last layer's combine step"
    x2 = x.reshape(B * S, D)
    out = _layer(x2, B, S, norm_mix[0], w_in[0], w_pool[0], pool_scale[0], w_attn_up[0], w_pool_up[0],
                 w_out[0], norm_ffn[0], w_router[0], b_router[0], w_gate_up[0], b_gate_up[0],
                 w_down[0], b_down[0], norm_final)
    return out.reshape(B, S, D)
```
